```python
import math
import functools
import jax
import jax.numpy as jnp
from jax import lax
import numpy as np

D_MODEL = 2048
BATCH = 1
SEQ = 8192
DEPTH = 1
DEC_BATCH = 128
DEC_SEQ = 1
PAST_LEN = 16384
PAGE_SIZE = 128

HEAD_DIM = 128
N_GDN_HEADS = D_MODEL // (2 * HEAD_DIM)
N_MLA_HEADS = D_MODEL // (2 * HEAD_DIM)
GDN_WIDTH = N_GDN_HEADS * HEAD_DIM
MLA_WIDTH = N_MLA_HEADS * HEAD_DIM
MIX_WIDTH = GDN_WIDTH + MLA_WIDTH
CONV_WIDTH = 4
GDN_CHUNK = 64
Q_RANK = D_MODEL // 4
KV_RANK = D_MODEL // 8
QK_NOPE_DIM = HEAD_DIM
ROPE_DIM = 64
V_DIM = HEAD_DIM
ROPE_THETA = 10000.0
MLA_SCALE = (QK_NOPE_DIM + ROPE_DIM) ** -0.5
Q_BLOCK = 128
PEER_HEADS = 8
PEER_N_KEYS = 128
PEER_N_EXPERTS = PEER_N_KEYS * PEER_N_KEYS
PEER_QUERY_DIM = 256
PEER_HALF = PEER_QUERY_DIM // 2
PEER_TOPK = 16
PEER_BLOCK = 128
EPS = 1e-6

OFF_Q = 0
OFF_K = OFF_Q + GDN_WIDTH
OFF_V = OFF_K + GDN_WIDTH
OFF_Z = OFF_V + GDN_WIDTH
OFF_B = OFF_Z + GDN_WIDTH
OFF_A = OFF_B + N_GDN_HEADS
OFF_CQ = OFF_A + N_GDN_HEADS
OFF_CKV = OFF_CQ + Q_RANK
OFF_KPE = OFF_CKV + KV_RANK
IN_COLS = OFF_KPE + ROPE_DIM

kernel_name = 'hymba_gdn_mla_peer_step'


def rmsnorm(x, g):
    xf = x.astype(jnp.float32)
    y = xf * lax.rsqrt(jnp.mean(xf * xf, axis=-1, keepdims=True) + EPS)
    return (y * g.astype(jnp.float32)).astype(x.dtype)


def l2norm(x):
    xf = x.astype(jnp.float32)
    return xf * lax.rsqrt(jnp.sum(xf * xf, axis=-1, keepdims=True) + EPS)


def rope(x, pos):
    half = ROPE_DIM // 2
    freqs = ROPE_THETA ** (-jnp.arange(half, dtype=jnp.float32) / half)
    ang = pos.astype(jnp.float32)[:, None] * freqs[None, :]
    cos = jnp.cos(ang)[None, :, None, :]
    sin = jnp.sin(ang)[None, :, None, :]
    xf = x.astype(jnp.float32)
    x1, x2 = xf[..., :half], xf[..., half:]
    return jnp.concatenate([x1 * cos - x2 * sin, x2 * cos + x1 * sin], axis=-1).astype(x.dtype)


def short_conv(x, buf, w):
    t = x.shape[1]
    xp = jnp.concatenate([buf.astype(x.dtype), x], axis=1)
    y = sum(xp[:, i:i + t] * w[i] for i in range(CONV_WIDTH))
    return jax.nn.silu(y), xp[:, t:]


def gdn_chunked(q, k, v, g, beta, s0):
    b, t, h, d = q.shape
    c = GDN_CHUNK
    n = t // c

    def chunks(a):
        return a.reshape((b, n, c, h) + a.shape[3:]).swapaxes(2, 3)

    qc, kc, vc, gc, bc = chunks(q), chunks(k), chunks(v), chunks(g), chunks(beta)
    gc = jnp.cumsum(gc, axis=-1)
    incl = jnp.tril(jnp.ones((c, c), bool))
    strict = jnp.tril(jnp.ones((c, c), bool), -1)
    decay = jnp.exp(jnp.where(incl, gc[..., :, None] - gc[..., None, :], -jnp.inf))
    kb = kc * bc[..., None]
    a_mat = jnp.where(strict, jnp.einsum('bnhik,bnhjk->bnhij', kb, kc) * decay, 0.0)
    m = a_mat + jnp.eye(c, dtype=jnp.float32)
    rhs = jnp.concatenate([vc * bc[..., None], kb * jnp.exp(gc)[..., None]], axis=-1)
    sol = lax.linalg.triangular_solve(m, rhs, left_side=True, lower=True, unit_diagonal=True)
    u, w = sol[..., :d], sol[..., d:]
    qk = jnp.where(incl, jnp.einsum('bnhik,bnhjk->bnhij', qc, kc) * decay, 0.0)

    def step(s, xs):
        qi, ki, ui, wi, gi, qki = xs
        v_new = ui - jnp.einsum('bhck,bhkv->bhcv', wi, s)
        o = (jnp.einsum('bhck,bhkv->bhcv', qi * jnp.exp(gi)[..., None], s)
             + jnp.einsum('bhij,bhjv->bhiv', qki, v_new))
        glast = gi[..., -1:]
        s = (s * jnp.exp(glast)[..., None]
             + jnp.einsum('bhck,bhcv->bhkv', ki * jnp.exp(glast - gi)[..., None], v_new))
        return s, o

    xs = tuple(jnp.moveaxis(a, 1, 0) for a in (qc, kc, u, w, gc, qk))
    s_fin, o = lax.scan(step, s0.astype(jnp.float32), xs)
    o = jnp.moveaxis(o, 0, 1).swapaxes(2, 3).reshape(b, t, h, d)
    return o, s_fin


def gdn_recurrent(q, k, v, g, beta, s0):
    def step(s, xs):
        qt, kt, vt, gt, bt = xs
        s = s * jnp.exp(gt)[..., None, None]
        delta = (vt - jnp.einsum('bhkv,bhk->bhv', s, kt)) * bt[..., None]
        s = s + jnp.einsum('bhk,bhv->bhkv', kt, delta)
        return s, jnp.einsum('bhkv,bhk->bhv', s, qt)

    xs = tuple(jnp.moveaxis(a, 1, 0) for a in (q, k, v, g, beta))
    s_fin, o = lax.scan(step, s0.astype(jnp.float32), xs)
    return jnp.moveaxis(o, 0, 1), s_fin


def mla_scores(q_lat, q_pe, ckv, kpe):
    s = jnp.einsum('bqhc,bkc->bhqk', q_lat, ckv) + jnp.einsum('bqhr,bkr->bhqk', q_pe, kpe)
    return s.astype(jnp.float32) * MLA_SCALE


def mla_prompt(q_lat, q_pe, ckv, kpe):
    b, t, h, c = q_lat.shape
    nb = t // Q_BLOCK
    key_pos = jnp.arange(t)

    def block(args):
        ql, qp, start = args
        s = mla_scores(ql, qp, ckv, kpe)
        mask = (start + jnp.arange(Q_BLOCK))[:, None] >= key_pos[None, :]
        p = jax.nn.softmax(jnp.where(mask, s, -jnp.inf), axis=-1)
        return jnp.einsum('bhqk,bkc->bqhc', p.astype(ckv.dtype), ckv)

    qlb = q_lat.reshape(b, nb, Q_BLOCK, h, c).swapaxes(0, 1)
    qpb = q_pe.reshape(b, nb, Q_BLOCK, h, ROPE_DIM).swapaxes(0, 1)
    ctx = lax.map(block, (qlb, qpb, jnp.arange(nb) * Q_BLOCK))
    return ctx.swapaxes(0, 1).reshape(b, t, h, c)


def mla_sample(q_lat, q_pe, ckv_new, kpe_new, cache_ckv, cache_kpe, page_table, layer):
    t = q_lat.shape[1]
    past = page_table.shape[1] * cache_ckv.shape[2]
    mask = jnp.concatenate([jnp.ones((t, past), bool), jnp.tril(jnp.ones((t, t), bool))], axis=1)

    def one(args):
        ql, qp, cn, kn, pages = args
        ckv_all = jnp.concatenate([cache_ckv[layer, pages].reshape(past, KV_RANK).astype(cn.dtype), cn], axis=0)
        kpe_all = jnp.concatenate([cache_kpe[layer, pages].reshape(past, ROPE_DIM).astype(kn.dtype), kn], axis=0)
        s = mla_scores(ql[None], qp[None], ckv_all[None], kpe_all[None])[0]
        p = jax.nn.softmax(jnp.where(mask, s, -jnp.inf), axis=-1)
        return jnp.einsum('hqk,kc->qhc', p.astype(ckv_all.dtype), ckv_all)

    return lax.map(one, (q_lat, q_pe, ckv_new, kpe_new, page_table))


def peer_ffn(xn, w_q, sub_keys, u_tab, v_tab):
    b, t, d = xn.shape
    n = b * t
    nb = -(-n // PEER_BLOCK)
    xf = jnp.pad(xn.reshape(n, d), ((0, nb * PEER_BLOCK - n), (0, 0))).reshape(nb, PEER_BLOCK, d)

    def block(xb):
        q = jnp.einsum('nd,dhk->nhk', xb, w_q)
        s1 = jnp.einsum('nhk,hik->nhi', q[..., :PEER_HALF], sub_keys[0])
        s2 = jnp.einsum('nhk,hik->nhi', q[..., PEER_HALF:], sub_keys[1])
        t1, i1 = lax.top_k(s1, PEER_TOPK)
        t2, i2 = lax.top_k(s2, PEER_TOPK)
        cand = (t1[..., :, None] + t2[..., None, :]).reshape(PEER_BLOCK, PEER_HEADS, PEER_TOPK * PEER_TOPK)
        cidx = (i1[..., :, None] * PEER_N_KEYS + i2[..., None, :]).reshape(PEER_BLOCK, PEER_HEADS, PEER_TOPK * PEER_TOPK)
        top, sel = lax.top_k(cand, PEER_TOPK)
        idx = jnp.take_along_axis(cidx, sel, axis=-1)
        gate = jax.nn.softmax(top.astype(jnp.float32), axis=-1)
        act = jax.nn.gelu(jnp.einsum('nhed,nd->nhe', u_tab[idx], xb).astype(jnp.float32), approximate=False)
        return jnp.einsum('nhe,nhed->nd', (gate * act).astype(xb.dtype), v_tab[idx])

    out = lax.map(block, xf)
    return out.reshape(nb * PEER_BLOCK, d)[:n].reshape(b, t, d)


def mixer_block(xn, pos, conv_buf, s0, gdn_core, attend, w_in, conv_w, a_log, dt_bias, gdn_norm_g,
                q_norm_g, w_uq, w_uk, kv_norm_g, w_uv, w_out):
    b, t, _ = xn.shape
    proj = xn @ w_in
    qkv, conv_new = short_conv(proj[..., OFF_Q:OFF_Z], conv_buf, conv_w)
    qkv = qkv.reshape(b, t, 3, N_GDN_HEADS, HEAD_DIM)
    q = l2norm(qkv[:, :, 0]) * HEAD_DIM ** -0.5
    k = l2norm(qkv[:, :, 1])
    v = qkv[:, :, 2].astype(jnp.float32)
    beta = jax.nn.sigmoid(proj[..., OFF_B:OFF_A].astype(jnp.float32))
    g = -jnp.exp(a_log.astype(jnp.float32)) * jax.nn.softplus(
        proj[..., OFF_A:OFF_CQ].astype(jnp.float32) + dt_bias.astype(jnp.float32))
    o, s_new = gdn_core(q, k, v, g, beta, s0)
    z = proj[..., OFF_Z:OFF_B].reshape(b, t, N_GDN_HEADS, HEAD_DIM).astype(jnp.float32)
    o_gdn = (rmsnorm(o, gdn_norm_g) * jax.nn.silu(z)).reshape(b, t, GDN_WIDTH).astype(xn.dtype)
    cq = rmsnorm(proj[..., OFF_CQ:OFF_CKV], q_norm_g)
    qh = jnp.einsum('btr,rhd->bthd', cq, w_uq)
    q_pe = rope(qh[..., QK_NOPE_DIM:], pos)
    q_lat = jnp.einsum('bthd,chd->bthc', qh[..., :QK_NOPE_DIM], w_uk)
    ckv = rmsnorm(proj[..., OFF_CKV:OFF_KPE], kv_norm_g)
    kpe = rope(proj[:, :, None, OFF_KPE:IN_COLS], pos)[:, :, 0]
    ctx = attend(q_lat, q_pe, ckv, kpe)
    o_mla = jnp.einsum('bthc,chd->bthd', ctx, w_uv).reshape(b, t, MLA_WIDTH)
    out = jnp.concatenate([o_gdn, o_mla], axis=-1) @ w_out
    return out, conv_new, s_new, ckv, kpe


def layer_group(x, pos, conv_buf, s0, gdn_core, attend, attn_g, ffn_g, mw, pw):
    mix, conv_new, s_new, ckv, kpe = mixer_block(rmsnorm(x, attn_g), pos, conv_buf, s0, gdn_core, attend, *mw)
    h = x + mix
    y = h + peer_ffn(rmsnorm(h, ffn_g), *pw)
    return y, conv_new, s_new, ckv, kpe


def setup_inputs(seed: int = 0) -> dict:
    key = jax.random.key(seed)
    ks = jax.random.split(key, 24)
    f32 = jnp.float32
    n_pages = PAST_LEN // PAGE_SIZE
    n_used = DEC_BATCH * n_pages
    n_phys = n_used + max(1, n_used // 4)

    def nrm(k, shape, scale):
        return jax.random.normal(k, shape, f32) * scale

    def gain(k, shape):
        return 1.0 + 0.05 * jax.random.normal(k, shape, f32)

    page_table = jax.random.permutation(ks[6], n_phys)[:n_used].reshape(DEC_BATCH, n_pages).astype(jnp.int32)
    dt = jnp.exp(jax.random.uniform(ks[10], (DEPTH, N_GDN_HEADS), f32, math.log(1e-3), math.log(1e-1)))
    return {
        'x_prompt': nrm(ks[0], (BATCH, SEQ, D_MODEL), 1.0),
        'x_sample': nrm(ks[1], (DEC_BATCH, DEC_SEQ, D_MODEL), 1.0),
        'cache_ckv': nrm(ks[2], (DEPTH, n_phys, PAGE_SIZE, KV_RANK), 1.0),
        'cache_kpe': nrm(ks[3], (DEPTH, n_phys, PAGE_SIZE, ROPE_DIM), 1.0),
        'state_gdn': nrm(ks[4], (DEPTH, DEC_BATCH, N_GDN_HEADS, HEAD_DIM, HEAD_DIM), 0.1),
        'state_conv': nrm(ks[5], (DEPTH, DEC_BATCH, CONV_WIDTH - 1, 3 * GDN_WIDTH), 1.0),
        'page_table': page_table,
        'attn_norm_g': gain(ks[7], (DEPTH, D_MODEL)),
        'w_in': nrm(ks[8], (DEPTH, D_MODEL, IN_COLS), D_MODEL ** -0.5),
        'conv_w': nrm(ks[9], (DEPTH, CONV_WIDTH, 3 * GDN_WIDTH), CONV_WIDTH ** -0.5),
        'a_log': jnp.log(jax.random.uniform(ks[11], (DEPTH, N_GDN_HEADS), f32, 1.0, 16.0)),
        'dt_bias': dt + jnp.log(-jnp.expm1(-dt)),
        'gdn_norm_g': gain(ks[12], (DEPTH, HEAD_DIM)),
        'q_norm_g': gain(ks[13], (DEPTH, Q_RANK)),
        'w_uq': nrm(ks[14], (DEPTH, Q_RANK, N_MLA_HEADS, QK_NOPE_DIM + ROPE_DIM), Q_RANK ** -0.5),
        'w_uk': nrm(ks[15], (DEPTH, KV_RANK, N_MLA_HEADS, QK_NOPE_DIM), KV_RANK ** -0.5),
        'kv_norm_g': gain(ks[16], (DEPTH, KV_RANK)),
        'w_uv': nrm(ks[17], (DEPTH, KV_RANK, N_MLA_HEADS, V_DIM), KV_RANK ** -0.5),
        'w_out': nrm(ks[18], (DEPTH, MIX_WIDTH, D_MODEL), MIX_WIDTH ** -0.5),
        'ffn_norm_g': gain(ks[19], (DEPTH, D_MODEL)),
        'peer_w_q': nrm(ks[20], (DEPTH, D_MODEL, PEER_HEADS, PEER_QUERY_DIM), D_MODEL ** -0.5),
        'peer_sub_keys': nrm(ks[21], (DEPTH, 2, PEER_HEADS, PEER_N_KEYS, PEER_HALF), PEER_HALF ** -0.5),
        'peer_u': nrm(ks[22], (DEPTH, PEER_N_EXPERTS, D_MODEL), D_MODEL ** -0.5),
        'peer_v': nrm(ks[23], (DEPTH, PEER_N_EXPERTS, D_MODEL), PEER_HEADS ** -0.5),
        'final_norm_g': gain(jax.random.fold_in(key, 99), (D_MODEL,)),
    }


def reference(x_prompt, x_sample, cache_ckv, cache_kpe, state_gdn, state_conv, page_table,
              attn_norm_g, w_in, conv_w, a_log, dt_bias, gdn_norm_g, q_norm_g, w_uq, w_uk,
              kv_norm_g, w_uv, w_out, ffn_norm_g, peer_w_q, peer_sub_keys, peer_u, peer_v,
              final_norm_g):
    b, t = x_prompt.shape[:2]
    td = x_sample.shape[1]
    past_len = page_table.shape[1] * cache_ckv.shape[2]
    pos_p = jnp.arange(t)
    pos_s = past_len + jnp.arange(td)
    conv0 = jnp.zeros((b, CONV_WIDTH - 1, 3 * GDN_WIDTH), x_prompt.dtype)
    s0 = jnp.zeros((b, N_GDN_HEADS, HEAD_DIM, HEAD_DIM), jnp.float32)
    hp, hs = x_prompt, x_sample
    sg_p, cv_p, ck_p, kp_p = [], [], [], []
    sg_s, cv_s, ck_s, kp_s = [], [], [], []
    for l in range(DEPTH):
        mw = (w_in[l], conv_w[l], a_log[l], dt_bias[l], gdn_norm_g[l], q_norm_g[l], w_uq[l],
              w_uk[l], kv_norm_g[l], w_uv[l], w_out[l])
        pw = (peer_w_q[l], peer_sub_keys[l], peer_u[l], peer_v[l])
        attend_s = functools.partial(mla_sample, cache_ckv=cache_ckv, cache_kpe=cache_kpe,
                                     page_table=page_table, layer=l)
        hp, c1, s1, k1, r1 = layer_group(hp, pos_p, conv0, s0, gdn_chunked, mla_prompt,
                                         attn_norm_g[l], ffn_norm_g[l], mw, pw)
        hs, c2, s2, k2, r2 = layer_group(hs, pos_s, state_conv[l], state_gdn[l], gdn_recurrent, attend_s,
                                         attn_norm_g[l], ffn_norm_g[l], mw, pw)
        sg_p.append(s1); cv_p.append(c1); ck_p.append(k1); kp_p.append(r1)
        sg_s.append(s2); cv_s.append(c2); ck_s.append(k2); kp_s.append(r2)
    y_prompt = rmsnorm(hp, final_norm_g)
    y_sample = rmsnorm(hs, final_norm_g)
    state_gdn_prompt = jnp.stack(sg_p)
    state_conv_prompt = jnp.stack(cv_p)
    ckv_prompt = jnp.stack(ck_p)
    kpe_prompt = jnp.stack(kp_p)
    state_gdn_sample = jnp.stack(sg_s)
    state_conv_sample = jnp.stack(cv_s)
    ckv_sample = jnp.stack(ck_s)
    kpe_sample = jnp.stack(kp_s)
    return (y_prompt, y_sample, state_gdn_prompt, state_conv_prompt, ckv_prompt, kpe_prompt,
            state_gdn_sample, state_conv_sample, ckv_sample, kpe_sample)
```

```python
import functools
import math

import jax
import jax.numpy as jnp
from jax import lax
from jax.experimental import pallas as pl
from jax.experimental.pallas import tpu as pltpu

F32 = jnp.float32
BF16 = jnp.bfloat16
NEG_INF = float("-inf")

EPS = 1e-6
HEAD_DIM = 128
ROPE_DIM = 64
ROPE_THETA = 10000.0
GDN_CHUNK = 64
PEER_TOPK = 16
LANES = 128
VMEM_LIMIT_BYTES = 56 * 1024 * 1024


def _params(*semantics):
    return pltpu.CompilerParams(dimension_semantics=semantics, vmem_limit_bytes=VMEM_LIMIT_BYTES)


def _pick_tile(n, target, align=8):
    if n <= target:
        return n
    for t in range(target - target % align, 0, -align):
        if n % t == 0:
            return t
    return n


def _dot(a, b):
    return jnp.dot(a, b, preferred_element_type=F32)


def _dot_nt(a, b):
    return lax.dot_general(a, b, (((1,), (1,)), ((), ())), preferred_element_type=F32)


def _split_bf16(a):
    hi = a.astype(BF16)
    lo = (a - hi.astype(F32)).astype(BF16)
    return hi, lo


def _dot3(a, b):
    a_hi, a_lo = _split_bf16(a)
    b_hi, b_lo = _split_bf16(b)
    return _dot(a_hi, b_hi) + (_dot(a_hi, b_lo) + _dot(a_lo, b_hi))


def _split3_bf16(a):
    hi = a.astype(BF16)
    r = a - hi.astype(F32)
    mid = r.astype(BF16)
    lo = (r - mid.astype(F32)).astype(BF16)
    return hi, mid, lo


def _ones_dot(ones_b, x):
    hi, mid, lo = _split3_bf16(x)
    return _dot(ones_b, hi) + (_dot(ones_b, mid) + _dot(ones_b, lo))


def _dot_ones(x, ones_b):
    hi, mid, lo = _split3_bf16(x)
    return _dot(hi, ones_b) + (_dot(mid, ones_b) + _dot(lo, ones_b))


def _silu(x):
    return x * (1.0 / (1.0 + jnp.exp(-x)))


def _sigmoid(x):
    return 1.0 / (1.0 + jnp.exp(-x))


def _softplus(x):
    return jnp.maximum(x, 0.0) + jnp.log1p(jnp.exp(-jnp.abs(x)))


def _norm_matmul_body(x_ref, g_ref, w_ref, o_ref, *rest, emit_xn):
    if emit_xn:
        xn_out_ref, xn_scr = rest
    else:
        (xn_scr,) = rest

    @pl.when(pl.program_id(1) == 0)
    def _():
        x = x_ref[...]
        y = x * lax.rsqrt(jnp.mean(x * x, axis=-1, keepdims=True) + EPS)
        xn = (y * g_ref[...]).astype(BF16)
        xn_scr[...] = xn
        if emit_xn:
            xn_out_ref[...] = xn

    o_ref[...] = _dot(xn_scr[...], w_ref[...])


def norm_matmul(x, g, w, *, emit_xn=False, name="norm_matmul"):
    m, k = x.shape
    n = w.shape[1]
    tm = _pick_tile(m, 512)
    tn = _pick_tile(n, 2560, LANES)
    assert n % tn == 0 and tn % LANES == 0
    out_shape = [jax.ShapeDtypeStruct((m, n), F32)]
    out_specs = [pl.BlockSpec((tm, tn), lambda i, j: (i, j))]
    if emit_xn:
        out_shape.append(jax.ShapeDtypeStruct((m, k), BF16))
        out_specs.append(pl.BlockSpec((tm, k), lambda i, j: (i, 0)))
    res = pl.pallas_call(
        functools.partial(_norm_matmul_body, emit_xn=emit_xn),
        grid=(m // tm, n // tn),
        in_specs=[pl.BlockSpec((tm, k), lambda i, j: (i, 0)),
                  pl.BlockSpec((1, k), lambda i, j: (0, 0)),
                  pl.BlockSpec((k, tn), lambda i, j: (0, j))],
        out_specs=out_specs,
        out_shape=out_shape,
        scratch_shapes=[pltpu.VMEM((tm, k), BF16)],
        compiler_params=_params("parallel", "arbitrary"),
        name=name,
    )(x, g.reshape(1, k), w)
    return res if emit_xn else res[0]


def _head_proj_body(x_ref, w_ref, o_ref):
    o_ref[...] = _dot(x_ref[...], w_ref[0]).astype(o_ref.dtype)


def head_proj(x, w, *, name="head_proj"):
    m = x.shape[0]
    nh, din, dout = w.shape
    tm = _pick_tile(m, 1024)
    return pl.pallas_call(
        _head_proj_body,
        grid=(m // tm, nh),
        in_specs=[pl.BlockSpec((tm, din), lambda i, h: (i, h)),
                  pl.BlockSpec((1, din, dout), lambda i, h: (h, 0, 0))],
        out_specs=pl.BlockSpec((tm, dout), lambda i, h: (i, h)),
        out_shape=jax.ShapeDtypeStruct((m, nh * dout), BF16),
        compiler_params=_params("parallel", "parallel"),
        name=name,
    )(x, w)


def _mla_prep_body(cq_ref, ckv_ref, kr_ref, qg_ref, kvg_ref, wn_ref, wpa_ref, wpb_ref,
                   cosq_ref, sinq_ref, cosk_ref, sink_ref,
                   ckv_o, kpe_o, ckvb_o, kpeb_o, qn_o, qp_o):
    cq = cq_ref[...]
    cqn = (cq * lax.rsqrt(jnp.mean(cq * cq, axis=-1, keepdims=True) + EPS) * qg_ref[...]).astype(BF16)
    qn_o[...] = _dot(cqn, wn_ref[...]).astype(BF16)
    qp = _dot(cqn, wpa_ref[...]) * cosq_ref[...] + _dot(cqn, wpb_ref[...]) * sinq_ref[...]
    for h in range(qp_o.shape[0]):
        qp_o[h] = qp[:, h * ROPE_DIM:(h + 1) * ROPE_DIM].astype(BF16)
    c = ckv_ref[...]
    cn = c * lax.rsqrt(jnp.mean(c * c, axis=-1, keepdims=True) + EPS) * kvg_ref[...]
    ckv_o[...] = cn
    ckvb_o[...] = cn.astype(BF16)
    kr = kr_ref[...]
    kpe = kr[:, :ROPE_DIM] * cosk_ref[...] + kr[:, ROPE_DIM:] * sink_ref[...]
    kpe_o[...] = kpe
    kpeb_o[...] = kpe.astype(BF16)


def mla_prep(proj, lay, q_norm_g, kv_norm_g, w_nope, w_pe_a, w_pe_b, cosq, sinq, cosk, sink):
    m = proj.shape[0]
    qr, kvr = lay["q_rank"], lay["kv_rank"]
    nn, npe = w_nope.shape[1], w_pe_a.shape[1]
    tm = _pick_tile(m, 512)
    row = lambda i: (i, 0)
    fixed = lambda i: (0, 0)
    return pl.pallas_call(
        _mla_prep_body,
        grid=(m // tm,),
        in_specs=[pl.BlockSpec((tm, qr), lambda i: (i, lay["off_cq"] // qr)),
                  pl.BlockSpec((tm, kvr), lambda i: (i, lay["off_ckv"] // kvr)),
                  pl.BlockSpec((tm, LANES), lambda i: (i, lay["off_kr"] // LANES)),
                  pl.BlockSpec((1, qr), fixed), pl.BlockSpec((1, kvr), fixed),
                  pl.BlockSpec((qr, nn), fixed), pl.BlockSpec((qr, npe), fixed),
                  pl.BlockSpec((qr, npe), fixed),
                  pl.BlockSpec((tm, npe), row), pl.BlockSpec((tm, npe), row),
                  pl.BlockSpec((tm, ROPE_DIM), row), pl.BlockSpec((tm, ROPE_DIM), row)],
        out_specs=[pl.BlockSpec((tm, kvr), row), pl.BlockSpec((tm, ROPE_DIM), row),
                   pl.BlockSpec((tm, kvr), row), pl.BlockSpec((tm, ROPE_DIM), row),
                   pl.BlockSpec((tm, nn), row),
                   pl.BlockSpec((npe // ROPE_DIM, tm, ROPE_DIM), lambda i: (0, i, 0))],
        out_shape=[jax.ShapeDtypeStruct((m, kvr), F32), jax.ShapeDtypeStruct((m, ROPE_DIM), F32),
                   jax.ShapeDtypeStruct((m, kvr), BF16), jax.ShapeDtypeStruct((m, ROPE_DIM), BF16),
                   jax.ShapeDtypeStruct((m, nn), BF16),
                   jax.ShapeDtypeStruct((npe // ROPE_DIM, m, ROPE_DIM), BF16)],
        compiler_params=_params("parallel"),
        name="mla_prep",
    )(proj, proj, proj, q_norm_g.reshape(1, qr), kv_norm_g.reshape(1, kvr),
      w_nope, w_pe_a, w_pe_b, cosq, sinq, cosk, sink)


def _attn_prompt_body(ql_ref, qp_ref, ckv_ref, kpe_ref, o_ref, m_scr, l_scr, acc_scr, *, tq, tk, scale):
    i = pl.program_id(0)
    q_l = ql_ref[...]
    q_p = qp_ref[...]
    m_scr[...] = jnp.full(m_scr.shape, NEG_INF, F32)
    l_scr[...] = jnp.zeros(l_scr.shape, F32)
    acc_scr[...] = jnp.zeros(acc_scr.shape, F32)

    def block(j, masked):
        start = pl.multiple_of(j * tk, tk)
        k = ckv_ref[pl.ds(start, tk), :]
        kp = kpe_ref[pl.ds(start, tk), :]
        s = (_dot_nt(q_l, k) + _dot_nt(q_p, kp)) * scale
        if masked:
            rows = i * tq + lax.broadcasted_iota(jnp.int32, (tq, tk), 0)
            cols = start + lax.broadcasted_iota(jnp.int32, (tq, tk), 1)
            s = jnp.where(rows >= cols, s, NEG_INF)
        m_prev = m_scr[...]
        m_new = jnp.maximum(m_prev, jnp.max(s, axis=-1, keepdims=True))
        alpha = jnp.exp(m_prev - m_new)
        p = jnp.exp(s - m_new)
        l_scr[...] = alpha * l_scr[...] + jnp.sum(p, axis=-1, keepdims=True)
        acc_scr[...] = alpha * acc_scr[...] + _dot(p.astype(BF16), k)
        m_scr[...] = m_new

    n_full = i * (tq // tk)

    def full_body(j, carry):
        block(j, False)
        return carry

    lax.fori_loop(0, n_full, full_body, 0)
    for d in range(tq // tk):
        block(n_full + d, True)
    o_ref[...] = (acc_scr[...] / l_scr[...]).astype(o_ref.dtype)


def attn_prompt(q_lat, q_pe, ckv_b, kpe_b, nh, scale):
    t = q_lat.shape[0]
    kvr = ckv_b.shape[1]
    tq = _pick_tile(t, 512)
    tk = tq
    return pl.pallas_call(
        functools.partial(_attn_prompt_body, tq=tq, tk=tk, scale=scale),
        grid=(t // tq, nh),
        in_specs=[pl.BlockSpec((tq, kvr), lambda i, h: (i, h)),
                  pl.BlockSpec((None, tq, ROPE_DIM), lambda i, h: (h, i, 0)),
                  pl.BlockSpec((t, kvr), lambda i, h: (0, 0)),
                  pl.BlockSpec((t, ROPE_DIM), lambda i, h: (0, 0))],
        out_specs=pl.BlockSpec((tq, kvr), lambda i, h: (i, h)),
        out_shape=jax.ShapeDtypeStruct((t, nh * kvr), BF16),
        scratch_shapes=[pltpu.VMEM((tq, 1), F32), pltpu.VMEM((tq, 1), F32), pltpu.VMEM((tq, kvr), F32)],
        compiler_params=_params("parallel", "parallel"),
        name="attn_prompt",
    )(q_lat, q_pe, ckv_b, kpe_b)


def _attn_sample_body(pt_ref, ql_ref, qp_ref, cn_ref, kn_ref, ckv_hbm, kpe_hbm, o_ref,
                      ckv_buf, kpe_buf, sem, *, layer, n_chunks, chunk_pages, page, tk, scale):
    r = pl.program_id(0)
    n_req = pl.num_programs(0)

    def chunk_copies(req, c, slot):
        copies = []
        for p in range(chunk_pages):
            phys = pt_ref[req, c * chunk_pages + p]
            copies.append(pltpu.make_async_copy(
                ckv_hbm.at[layer, phys], ckv_buf.at[slot, pl.ds(p * page, page), :], sem.at[0, slot]))
            copies.append(pltpu.make_async_copy(
                kpe_hbm.at[layer, phys], kpe_buf.at[slot, pl.ds(p * page, page), :], sem.at[1, slot]))
        return copies

    def start_chunk(req, c, slot):
        for cp in chunk_copies(req, c, slot):
            cp.start()

    def wait_chunk(req, c, slot):
        for cp in chunk_copies(req, c, slot):
            cp.wait()

    @pl.when(r == 0)
    def _():
        start_chunk(0, 0, 0)

    q_l = ql_ref[0]
    q_p = qp_ref[0]
    hp, kvr = q_l.shape
    m = jnp.full((hp, 1), NEG_INF, F32)
    l = jnp.zeros((hp, 1), F32)
    acc = jnp.zeros((hp, kvr), F32)

    for c in range(n_chunks):
        slot = c % 2
        if c + 1 < n_chunks:
            start_chunk(r, c + 1, (c + 1) % 2)
        else:
            @pl.when(r + 1 < n_req)
            def _():
                start_chunk(r + 1, 0, 0)
        wait_chunk(r, c, slot)

        def sub(j, carry):
            m, l, acc = carry
            start = pl.multiple_of(j * tk, tk)
            k = ckv_buf[slot, pl.ds(start, tk), :].astype(BF16)
            kp = kpe_buf[slot, pl.ds(start, tk), :].astype(BF16)
            s = (_dot_nt(q_l, k) + _dot_nt(q_p, kp)) * scale
            m_new = jnp.maximum(m, jnp.max(s, axis=-1, keepdims=True))
            alpha = jnp.exp(m - m_new)
            p = jnp.exp(s - m_new)
            l = alpha * l + jnp.sum(p, axis=-1, keepdims=True)
            acc = alpha * acc + _dot(p.astype(BF16), k)
            return m_new, l, acc

        m, l, acc = lax.fori_loop(0, chunk_pages * page // tk, sub, (m, l, acc))

    cn = cn_ref[0].astype(BF16).astype(F32)
    kn = kn_ref[0].astype(BF16).astype(F32)
    s_self = (jnp.sum(q_l.astype(F32) * cn, axis=-1, keepdims=True)
              + jnp.sum(q_p.astype(F32) * kn, axis=-1, keepdims=True)) * scale
    m_new = jnp.maximum(m, s_self)
    alpha = jnp.exp(m - m_new)
    p_self = jnp.exp(s_self - m_new)
    l = alpha * l + p_self
    acc = alpha * acc + p_self.astype(BF16).astype(F32) * cn
    o_ref[0] = acc / l


def attn_sample(q_lat, q_pe, ckv_new, kpe_new, cache_ckv, cache_kpe, page_table, layer, scale):
    nb, nh, kvr = q_lat.shape
    n_pages = page_table.shape[1]
    page = cache_ckv.shape[2]
    chunk_pages = max(1, min(32, n_pages // 2))
    while n_pages % chunk_pages:
        chunk_pages -= 1
    n_chunks = n_pages // chunk_pages
    assert n_chunks % 2 == 0, "slot parity assumes an even chunk count"
    rows = chunk_pages * page
    tk = _pick_tile(rows, 512)
    grid_spec = pltpu.PrefetchScalarGridSpec(
        num_scalar_prefetch=1,
        grid=(nb,),
        in_specs=[pl.BlockSpec((1, nh, kvr), lambda r, pt: (r, 0, 0)),
                  pl.BlockSpec((1, nh, ROPE_DIM), lambda r, pt: (r, 0, 0)),
                  pl.BlockSpec((1, 1, kvr), lambda r, pt: (r, 0, 0)),
                  pl.BlockSpec((1, 1, ROPE_DIM), lambda r, pt: (r, 0, 0)),
                  pl.BlockSpec(memory_space=pl.ANY),
                  pl.BlockSpec(memory_space=pl.ANY)],
        out_specs=pl.BlockSpec((1, nh, kvr), lambda r, pt: (r, 0, 0)),
        scratch_shapes=[pltpu.VMEM((2, rows, kvr), F32),
                        pltpu.VMEM((2, rows, ROPE_DIM), F32),
                        pltpu.SemaphoreType.DMA((2, 2))],
    )
    return pl.pallas_call(
        functools.partial(_attn_sample_body, layer=layer, n_chunks=n_chunks, chunk_pages=chunk_pages,
                          page=page, tk=tk, scale=scale),
        grid_spec=grid_spec,
        out_shape=jax.ShapeDtypeStruct((nb, nh, kvr), F32),
        compiler_params=_params("arbitrary"),
        name="attn_sample",
    )(page_table, q_lat, q_pe, ckv_new, kpe_new, cache_ckv, cache_kpe)


def _gdn_prompt_body(q_ref, k_ref, v_ref, z_ref, ba_ref, cw_ref, alog_ref, dtb_ref, gng_ref,
                     o_ref, state_ref,
                     tail_scr, xp_scr, q_scr, k_scr, v_scr, g_scr, o_scr, s_scr, *, nh, rows):
    i = pl.program_id(0)
    c = GDN_CHUNK
    d = HEAD_DIM
    gw = nh * d
    kw = cw_ref.shape[0]

    @pl.when(i == 0)
    def _():
        tail_scr[...] = jnp.zeros(tail_scr.shape, F32)
        s_scr[...] = jnp.zeros(s_scr.shape, F32)

    for part, (src, dst) in enumerate(((q_ref, q_scr), (k_ref, k_scr), (v_ref, v_scr))):
        xp_scr[0:8, :] = tail_scr[part]
        xp_scr[8:8 + rows, :] = src[...]
        acc = jnp.zeros((rows, gw), F32)
        for tap in range(kw):
            lo = 8 - (kw - 1) + tap
            acc = acc + xp_scr[lo:lo + rows, :] * cw_ref[tap:tap + 1, part * gw:(part + 1) * gw]
        tail_scr[part] = xp_scr[rows:rows + 8, :]
        y = _silu(acc)
        if part < 2:
            for h in range(nh):
                yh = y[:, h * d:(h + 1) * d]
                yn = yh * lax.rsqrt(jnp.sum(yh * yh, axis=-1, keepdims=True) + EPS)
                if part == 0:
                    yn = yn * (d ** -0.5)
                dst[:, h * d:(h + 1) * d] = yn
        else:
            dst[...] = y

    ba = ba_ref[...]
    lane = lax.broadcasted_iota(jnp.int32, ba.shape, 1)
    g_scr[...] = jnp.where(lane < nh, _sigmoid(ba),
                           -jnp.exp(alog_ref[...]) * _softplus(ba + dtb_ref[...]))

    ri = lax.broadcasted_iota(jnp.int32, (c, c), 0)
    ci = lax.broadcasted_iota(jnp.int32, (c, c), 1)
    incl = ri >= ci
    strict = ri > ci
    lower_ones = jnp.where(incl, 1.0, 0.0).astype(BF16)
    upper_ones = jnp.where(ri <= ci, 1.0, 0.0).astype(BF16)
    eye = jnp.where(ri == ci, 1.0, 0.0)

    def chunk_body(n, carry):
        r0 = pl.multiple_of(n * c, c)
        gblk = g_scr[pl.ds(r0, c), :]
        gcol = _ones_dot(lower_ones, gblk)
        grow = _dot_ones(gblk.T[nh:2 * nh, :], upper_ones)
        for h in range(nh):
            sl = slice(h * d, (h + 1) * d)
            q = q_scr[pl.ds(r0, c), sl]
            k = k_scr[pl.ds(r0, c), sl]
            v = v_scr[pl.ds(r0, c), sl]
            beta = gblk[:, h:h + 1]
            gc = gcol[:, nh + h:nh + h + 1]
            gr = grow[h:h + 1, :]
            decay = jnp.exp(jnp.where(incl, gc - gr, NEG_INF))
            kb = k * beta
            k_b = k.astype(BF16)
            a_mat = jnp.where(strict, _dot_nt(kb.astype(BF16), k_b) * decay, 0.0)
            t_inv = eye - a_mat
            pw = a_mat
            for _ in range(int(math.log2(c)) - 1):
                pw = _dot3(pw, pw)
                t_inv = t_inv + _dot3(t_inv, pw)
            rhs = jnp.concatenate([v * beta, kb * jnp.exp(gc)], axis=-1)
            sol = _dot3(t_inv, rhs)
            u, w = sol[:, :d], sol[:, d:]
            qk = jnp.where(incl, _dot_nt(q.astype(BF16), k_b) * decay, 0.0)
            s = s_scr[h]
            s_b = s.astype(BF16)
            v_new = u - _dot(w.astype(BF16), s_b)
            v_new_b = v_new.astype(BF16)
            o = _dot((q * jnp.exp(gc)).astype(BF16), s_b) + _dot(qk.astype(BF16), v_new_b)
            glast = gc[c - 1:c, :]
            kg_t = (k * jnp.exp(glast - gc)).T.astype(BF16)
            s_scr[h] = s * jnp.exp(glast) + _dot(kg_t, v_new_b)
            o_scr[pl.ds(r0, c), sl] = o
        return carry

    lax.fori_loop(0, rows // c, chunk_body, 0)

    for h in range(nh):
        sl = slice(h * d, (h + 1) * d)
        oh = o_scr[:, sl]
        on = oh * lax.rsqrt(jnp.mean(oh * oh, axis=-1, keepdims=True) + EPS) * gng_ref[...]
        o_ref[:, sl] = (on * _silu(z_ref[:, sl])).astype(o_ref.dtype)

    @pl.when(i == pl.num_programs(0) - 1)
    def _():
        state_ref[...] = s_scr[...]


def gdn_prompt(proj, lay, conv_w, a_log_l, dt_bias_l, gdn_norm_g, nh):
    t = proj.shape[0]
    gw = nh * HEAD_DIM
    rows = _pick_tile(t, 256)
    assert rows % GDN_CHUNK == 0 and rows >= 8
    kw = conv_w.shape[0]
    assert kw - 1 <= 8
    fixed = lambda i: (0, 0)
    return pl.pallas_call(
        functools.partial(_gdn_prompt_body, nh=nh, rows=rows),
        grid=(t // rows,),
        in_specs=[pl.BlockSpec((rows, gw), lambda i: (i, 0)),
                  pl.BlockSpec((rows, gw), lambda i: (i, 1)),
                  pl.BlockSpec((rows, gw), lambda i: (i, 2)),
                  pl.BlockSpec((rows, gw), lambda i: (i, 3)),
                  pl.BlockSpec((rows, LANES), lambda i: (i, lay["off_ba"] // LANES)),
                  pl.BlockSpec((kw, 3 * gw), fixed),
                  pl.BlockSpec((1, LANES), fixed), pl.BlockSpec((1, LANES), fixed),
                  pl.BlockSpec((1, HEAD_DIM), fixed)],
        out_specs=[pl.BlockSpec((rows, gw), lambda i: (i, 0)),
                   pl.BlockSpec((nh, HEAD_DIM, HEAD_DIM), lambda i: (0, 0, 0))],
        out_shape=[jax.ShapeDtypeStruct((t, gw), BF16),
                   jax.ShapeDtypeStruct((nh, HEAD_DIM, HEAD_DIM), F32)],
        scratch_shapes=[pltpu.VMEM((3, 8, gw), F32), pltpu.VMEM((rows + 8, gw), F32),
                        pltpu.VMEM((rows, gw), F32), pltpu.VMEM((rows, gw), F32),
                        pltpu.VMEM((rows, gw), F32), pltpu.VMEM((rows, LANES), F32),
                        pltpu.VMEM((rows, gw), F32), pltpu.VMEM((nh, HEAD_DIM, HEAD_DIM), F32)],
        compiler_params=_params("arbitrary"),
        name="gdn_prompt",
    )(proj, proj, proj, proj, proj, conv_w, a_log_l, dt_bias_l, gdn_norm_g.reshape(1, HEAD_DIM))


def _gdn_sample_prep_body(q_ref, k_ref, v_ref, ba_ref, sc_ref, cw_ref, alog_ref, dtb_ref,
                          qn_o, kn_o, v_o, g_o, sc_o, *, nh):
    d = HEAD_DIM
    gw = nh * d
    kw = cw_ref.shape[0]
    width = 3 * gw
    for part, (src, dst) in enumerate(((q_ref, qn_o), (k_ref, kn_o), (v_ref, v_o))):
        x = src[...]
        acc = x * cw_ref[kw - 1:kw, part * gw:(part + 1) * gw]
        for tap in range(kw - 1):
            lo = tap * width + part * gw
            acc = acc + sc_ref[:, lo:lo + gw] * cw_ref[tap:tap + 1, part * gw:(part + 1) * gw]
        for tap in range(kw - 1):
            dst_lo = tap * width + part * gw
            if tap + 1 < kw - 1:
                src_lo = (tap + 1) * width + part * gw
                sc_o[:, dst_lo:dst_lo + gw] = sc_ref[:, src_lo:src_lo + gw]
            else:
                sc_o[:, dst_lo:dst_lo + gw] = x
        y = _silu(acc)
        if part < 2:
            for h in range(nh):
                yh = y[:, h * d:(h + 1) * d]
                yn = yh * lax.rsqrt(jnp.sum(yh * yh, axis=-1, keepdims=True) + EPS)
                if part == 0:
                    yn = yn * (d ** -0.5)
                dst[:, h * d:(h + 1) * d] = yn
        else:
            dst[...] = y
    ba = ba_ref[...]
    lane = lax.broadcasted_iota(jnp.int32, ba.shape, 1)
    g_o[...] = jnp.where(lane < nh, _sigmoid(ba),
                         -jnp.exp(alog_ref[...]) * _softplus(ba + dtb_ref[...]))


def gdn_sample_prep(proj, lay, state_conv_flat, conv_w, a_log_l, dt_bias_l, nh):
    nb = proj.shape[0]
    gw = nh * HEAD_DIM
    kw = conv_w.shape[0]
    fixed = lambda i: (0, 0)
    return pl.pallas_call(
        functools.partial(_gdn_sample_prep_body, nh=nh),
        grid=(1,),
        in_specs=[pl.BlockSpec((nb, gw), lambda i: (0, 0)),
                  pl.BlockSpec((nb, gw), lambda i: (0, 1)),
                  pl.BlockSpec((nb, gw), lambda i: (0, 2)),
                  pl.BlockSpec((nb, LANES), lambda i: (0, lay["off_ba"] // LANES)),
                  pl.BlockSpec((nb, (kw - 1) * 3 * gw), fixed),
                  pl.BlockSpec((kw, 3 * gw), fixed),
                  pl.BlockSpec((1, LANES), fixed), pl.BlockSpec((1, LANES), fixed)],
        out_specs=[pl.BlockSpec((nb, gw), fixed), pl.BlockSpec((nb, gw), fixed),
                   pl.BlockSpec((nb, gw), fixed), pl.BlockSpec((nb, LANES), fixed),
                   pl.BlockSpec((nb, (kw - 1) * 3 * gw), fixed)],
        out_shape=[jax.ShapeDtypeStruct((nb, gw), F32), jax.ShapeDtypeStruct((nb, gw), F32),
                   jax.ShapeDtypeStruct((nb, gw), F32), jax.ShapeDtypeStruct((nb, LANES), F32),
                   jax.ShapeDtypeStruct((nb, (kw - 1) * 3 * gw), F32)],
        compiler_params=_params("arbitrary"),
        name="gdn_sample_prep",
    )(proj, proj, proj, proj, state_conv_flat, conv_w, a_log_l, dt_bias_l)


def _gdn_sample_step_body(s_ref, q_ref, k_ref, v_ref, z_ref, g_ref, gng_ref, s_o, o_o, *, nh, rb):
    for b in range(rb):
        kq_t = jnp.concatenate([k_ref[b], q_ref[b]], axis=0).T
        gb = g_ref[b]
        for h in range(nh):
            kcol = kq_t[:, h:h + 1]
            qcol = kq_t[:, nh + h:nh + h + 1]
            beta = gb[:, h:h + 1]
            decay = jnp.exp(gb[:, nh + h:nh + h + 1])
            s = s_ref[b, h] * decay
            sk = jnp.sum(s * kcol, axis=0, keepdims=True)
            delta = (v_ref[b, h:h + 1, :] - sk) * beta
            s = s + kcol * delta
            s_o[b, h] = s
            o = jnp.sum(s * qcol, axis=0, keepdims=True)
            on = o * lax.rsqrt(jnp.mean(o * o, axis=-1, keepdims=True) + EPS) * gng_ref[...]
            o_o[b, h:h + 1, :] = (on * _silu(z_ref[b, h:h + 1, :])).astype(o_o.dtype)


def gdn_sample_step(state, qn, kn, v, z, g, gdn_norm_g):
    nb, nh, d, _ = state.shape
    rb = 4 if nb % 4 == 0 else 1
    vec = pl.BlockSpec((rb, nh, d), lambda i: (i, 0, 0))
    return pl.pallas_call(
        functools.partial(_gdn_sample_step_body, nh=nh, rb=rb),
        grid=(nb // rb,),
        in_specs=[pl.BlockSpec((rb, nh, d, d), lambda i: (i, 0, 0, 0)), vec, vec, vec, vec,
                  pl.BlockSpec((rb, 1, LANES), lambda i: (i, 0, 0)),
                  pl.BlockSpec((1, d), lambda i: (0, 0))],
        out_specs=[pl.BlockSpec((rb, nh, d, d), lambda i: (i, 0, 0, 0)),
                   pl.BlockSpec((rb, nh, d), lambda i: (i, 0, 0))],
        out_shape=[jax.ShapeDtypeStruct(state.shape, F32), jax.ShapeDtypeStruct((nb, nh, d), BF16)],
        compiler_params=_params("parallel"),
        name="gdn_sample_step",
    )(state, qn, kn, v, z, g, gdn_norm_g.reshape(1, d))


def _out_proj_body(a_ref, b_ref, wa_ref, wb_ref, x_ref, o_ref):
    o_ref[...] = x_ref[...] + (_dot(a_ref[...], wa_ref[...]) + _dot(b_ref[...], wb_ref[...]))


def out_proj(o_gdn, o_mla, w_a, w_b, x):
    m, ka = o_gdn.shape
    kb = o_mla.shape[1]
    n = w_a.shape[1]
    tm = _pick_tile(m, 512)
    return pl.pallas_call(
        _out_proj_body,
        grid=(m // tm,),
        in_specs=[pl.BlockSpec((tm, ka), lambda i: (i, 0)), pl.BlockSpec((tm, kb), lambda i: (i, 0)),
                  pl.BlockSpec((ka, n), lambda i: (0, 0)), pl.BlockSpec((kb, n), lambda i: (0, 0)),
                  pl.BlockSpec((tm, n), lambda i: (i, 0))],
        out_specs=pl.BlockSpec((tm, n), lambda i: (i, 0)),
        out_shape=jax.ShapeDtypeStruct((m, n), F32),
        compiler_params=_params("parallel"),
        name="out_proj",
    )(o_gdn, o_mla, w_a, w_b, x)


def _top16(s, iota_k):
    n = s.shape[0]
    row16 = lax.broadcasted_iota(jnp.int32, (PEER_TOPK, s.shape[1]), 0)
    rank = jnp.full(s.shape, float(PEER_TOPK), F32)
    vals = jnp.zeros((PEER_TOPK, s.shape[1]), F32)
    for a in range(PEER_TOPK):
        mx = jnp.max(s, axis=0, keepdims=True)
        first = jnp.min(jnp.where(s == mx, iota_k, n), axis=0, keepdims=True)
        hit = iota_k == first
        rank = jnp.where(hit, float(a), rank)
        s = jnp.where(hit, NEG_INF, s)
        vals = jnp.where(row16 == a, mx, vals)
    return vals, rank


def _peer_topk_body(q_ref, k1_ref, k2_ref, e1_o, c1_o, e2_o, r2_o, *, nh, half):
    tn = q_ref.shape[0]
    nk = k1_ref.shape[1]
    kk = PEER_TOPK
    iota_k = lax.broadcasted_iota(jnp.int32, (nk, tn), 0)
    row8 = lax.broadcasted_iota(jnp.int32, (kk // 2, tn), 0)
    for h in range(nh):
        qh = q_ref[:, h * 2 * half:(h + 1) * 2 * half]
        s1 = lax.dot_general(k1_ref[h], qh[:, :half], (((1,), (1,)), ((), ())),
                             preferred_element_type=F32, precision=lax.Precision.HIGHEST)
        s2 = lax.dot_general(k2_ref[h], qh[:, half:], (((1,), (1,)), ((), ())),
                             preferred_element_type=F32, precision=lax.Precision.HIGHEST)
        t1, rank1 = _top16(s1, iota_k)
        t2, rank2 = _top16(s2, iota_k)
        hk = kk // 2
        top1 = t1[0:1, :]
        top2 = t2[0:1, :]
        ex1 = jnp.exp(t1 - top1)
        ex2 = jnp.exp(t2 - top2)
        blocks = [t1[0:1, :] + t2[:hk, :], t1[0:1, :] + t2[hk:, :]]
        flat = [row8, row8 + hk]
        gates = [ex2[:hk, :], ex2[hk:, :]]
        for a in range(1, hk):
            blocks.append(jnp.where(row8 < kk // (a + 1), t1[a:a + 1, :] + t2[:hk, :], NEG_INF))
            flat.append(row8 + a * kk)
            gates.append(ex1[a:a + 1, :] * ex2[:hk, :])
        blocks.append(t1[hk:, :] + t2[0:1, :])
        flat.append((row8 + hk) * kk)
        gates.append(ex1[hk:, :])
        nblk = len(blocks)
        sel = [jnp.zeros((hk, tn), F32) for _ in range(nblk)]
        for _ in range(kk):
            mx = blocks[0]
            for a in range(1, nblk):
                mx = jnp.maximum(mx, blocks[a])
            mx = jnp.max(mx, axis=0, keepdims=True)
            first = jnp.where(blocks[0] == mx, flat[0], kk * kk)
            for a in range(1, nblk):
                first = jnp.minimum(first, jnp.where(blocks[a] == mx, flat[a], kk * kk))
            first = jnp.min(first, axis=0, keepdims=True)
            for a in range(nblk):
                hit = flat[a] == first
                sel[a] = jnp.where(hit, 1.0, sel[a])
                blocks[a] = jnp.where(hit, NEG_INF, blocks[a])
        z = sel[0] * gates[0]
        for a in range(1, nblk):
            z = z + sel[a] * gates[a]
        inv_z = 1.0 / jnp.sum(z, axis=0, keepdims=True)
        counts = [jnp.sum(sel[0] + sel[1], axis=0, keepdims=True)]
        for a in range(1, hk):
            counts.append(jnp.sum(sel[a + 1], axis=0, keepdims=True))
        for a in range(hk, kk):
            counts.append(sel[nblk - 1][a - hk:a - hk + 1, :])
        c1 = jnp.zeros((nk, tn), F32)
        for a in range(kk):
            c1 = jnp.where(rank1 == float(a), counts[a], c1)
        e1_o[h] = jnp.where(rank1 < float(kk), jnp.exp(s1 - top1) * inv_z, 0.0)
        c1_o[h] = c1
        e2_o[h] = jnp.where(rank2 < float(kk), jnp.exp(s2 - top2), 0.0)
        r2_o[h] = rank2


def peer_topk(pq, keys1, keys2):
    t = pq.shape[0]
    nh, nk, half = keys1.shape
    tn = _pick_tile(t, 256)
    assert tn % LANES == 0
    out = jax.ShapeDtypeStruct((nh, nk, t), F32)
    ospec = pl.BlockSpec((nh, nk, tn), lambda i: (0, 0, i))
    kspec = pl.BlockSpec((nh, nk, half), lambda i: (0, 0, 0))
    return pl.pallas_call(
        functools.partial(_peer_topk_body, nh=nh, half=half),
        grid=(t // tn,),
        in_specs=[pl.BlockSpec((tn, nh * 2 * half), lambda i: (i, 0)), kspec, kspec],
        out_specs=[ospec] * 4,
        out_shape=[out] * 4,
        compiler_params=_params("parallel"),
        name="peer_topk",
    )(pq, keys1, keys2)


def _gelu_exact(x):
    return 0.5 * x * (1.0 + lax.erf(x * (2.0 ** -0.5)))


def _peer_dense_body(xn_ref, u_ref, vt_ref, e1_ref, c1_ref, e2_ref, r2_ref, h_ref, fg_ref, y_ref,
                     acc_scr, *, nh, nk, rows_per_tile, final_norm):
    j = pl.program_id(1)

    @pl.when(j == 0)
    def _():
        acc_scr[...] = jnp.zeros(acc_scr.shape, F32)

    st = _dot_nt(u_ref[...], xn_ref[...])
    act = _gelu_exact(st)
    parts = []
    for r in range(rows_per_tile):
        i1 = j * rows_per_tile + r
        gate = jnp.zeros((nk, st.shape[1]), F32)
        for h in range(nh):
            e1 = e1_ref[h, pl.ds(i1, 1), :]
            c1 = c1_ref[h, pl.ds(i1, 1), :]
            gate = gate + jnp.where(r2_ref[h] < c1, e2_ref[h], 0.0) * e1
        parts.append((gate * act[r * nk:(r + 1) * nk, :]).astype(BF16))
    wt = jnp.concatenate(parts, axis=0) if len(parts) > 1 else parts[0]
    acc_scr[...] += _dot(vt_ref[...], wt)

    @pl.when(j == pl.num_programs(1) - 1)
    def _():
        y = h_ref[...] + acc_scr[...].T
        if final_norm:
            y = y * lax.rsqrt(jnp.mean(y * y, axis=-1, keepdims=True) + EPS) * fg_ref[...]
        y_ref[...] = y


def peer_dense(xn, u_tab, v_tab_t, e1, c1, e2, r2, h, final_g, final_norm):
    t, dm = xn.shape
    ne = u_tab.shape[0]
    nh, nk, _ = e1.shape
    tn = _pick_tile(t, 512)
    rows_per_tile = 4
    te = rows_per_tile * nk
    assert ne == nk * nk and ne % te == 0 and tn % LANES == 0
    gspec = pl.BlockSpec((nh, nk, tn), lambda i, j: (0, 0, i))
    return pl.pallas_call(
        functools.partial(_peer_dense_body, nh=nh, nk=nk, rows_per_tile=rows_per_tile,
                          final_norm=final_norm),
        grid=(t // tn, ne // te),
        in_specs=[pl.BlockSpec((tn, dm), lambda i, j: (i, 0)),
                  pl.BlockSpec((te, dm), lambda i, j: (j, 0)),
                  pl.BlockSpec((dm, te), lambda i, j: (0, j)),
                  gspec, gspec, gspec, gspec,
                  pl.BlockSpec((tn, dm), lambda i, j: (i, 0)),
                  pl.BlockSpec((1, dm), lambda i, j: (0, 0))],
        out_specs=pl.BlockSpec((tn, dm), lambda i, j: (i, 0)),
        out_shape=jax.ShapeDtypeStruct((t, dm), F32),
        scratch_shapes=[pltpu.VMEM((dm, tn), F32)],
        compiler_params=_params("parallel", "arbitrary"),
        name="peer_dense",
    )(xn, u_tab, v_tab_t, e1, c1, e2, r2, h, final_g.reshape(1, dm))


def _swap_halves(w):
    lead = w.shape[:-1]
    g = w.reshape(lead + (-1, 2, ROPE_DIM // 2))
    return g[..., ::-1, :].reshape(w.shape)


def _pack_w_in(w_in, nh, q_rank, kv_rank):
    gw = nh * HEAD_DIM
    off_b = 4 * gw
    off_a = off_b + nh
    off_cq = off_a + nh
    off_ckv = off_cq + q_rank
    off_kpe = off_ckv + kv_rank
    assert w_in.shape[1] == off_kpe + ROPE_DIM and 2 * nh <= LANES
    w_kpe = w_in[:, off_kpe:]
    parts = [w_in[:, :off_b], w_in[:, off_cq:off_kpe], w_kpe, _swap_halves(w_kpe),
             w_in[:, off_b:off_cq], jnp.zeros((w_in.shape[0], LANES - 2 * nh), w_in.dtype)]
    lay = {"off_cq": 4 * gw, "off_ckv": 4 * gw + q_rank, "off_kr": 4 * gw + q_rank + kv_rank,
           "off_ba": 4 * gw + q_rank + kv_rank + 2 * ROPE_DIM, "q_rank": q_rank, "kv_rank": kv_rank}
    assert lay["off_cq"] % q_rank == 0 and lay["off_ckv"] % kv_rank == 0
    return jnp.concatenate(parts, axis=1).astype(BF16), lay


def _rope_tables(pos, nh):
    half = ROPE_DIM // 2
    freqs = ROPE_THETA ** (-jnp.arange(half, dtype=F32) / half)
    ang = pos.astype(F32)[:, None] * freqs[None, :]
    cos, sin = jnp.cos(ang), jnp.sin(ang)
    cosk = jnp.concatenate([cos, cos], axis=-1)
    sink = jnp.concatenate([-sin, sin], axis=-1)
    return jnp.tile(cosk, (1, nh)), jnp.tile(sink, (1, nh)), cosk, sink


def _lane_row(vec, offset):
    return jnp.zeros((1, LANES), F32).at[0, offset:offset + vec.shape[0]].set(vec.astype(F32))


def kernel(x_prompt, x_sample, cache_ckv, cache_kpe, state_gdn, state_conv, page_table,
           attn_norm_g, w_in, conv_w, a_log, dt_bias, gdn_norm_g, q_norm_g, w_uq, w_uk,
           kv_norm_g, w_uv, w_out, ffn_norm_g, peer_w_q, peer_sub_keys, peer_u, peer_v,
           final_norm_g):
    b, t, dm = x_prompt.shape
    nb, td, _ = x_sample.shape
    assert b == 1 and td == 1, "one prompt sequence and one new token per request"
    depth = w_in.shape[0]
    nh_g = a_log.shape[1]
    gw = nh_g * HEAD_DIM
    q_rank, nh_m, qk_dim = w_uq.shape[1:]
    nope = qk_dim - ROPE_DIM
    kv_rank = w_uk.shape[1]
    v_dim = w_uv.shape[3]
    scale = float(qk_dim) ** -0.5
    past_len = page_table.shape[1] * cache_ckv.shape[2]
    kw = conv_w.shape[1]
    p_heads, p_qdim = peer_w_q.shape[2:]

    cosq_p, sinq_p, cosk_p, sink_p = _rope_tables(jnp.arange(t), nh_m)
    cq1, sq1, ck1, sk1 = _rope_tables(past_len + jnp.arange(td), nh_m)
    cosq_s, sinq_s, cosk_s, sink_s = (jnp.broadcast_to(a, (nb, a.shape[1])) for a in (cq1, sq1, ck1, sk1))

    hp = x_prompt.reshape(t, dm)
    hs = x_sample.reshape(nb, dm)
    outs = {k: [] for k in ("sg_p", "cv_p", "ck_p", "kp_p", "sg_s", "cv_s", "ck_s", "kp_s")}
    for l in range(depth):
        w_in_p, lay = _pack_w_in(w_in[l], nh_g, q_rank, kv_rank)
        a_log_l = _lane_row(a_log[l], nh_g)
        dt_bias_l = _lane_row(dt_bias[l], nh_g)
        w_nope = w_uq[l][:, :, :nope].reshape(q_rank, nh_m * nope).astype(BF16)
        w_pe = w_uq[l][:, :, nope:].reshape(q_rank, nh_m * ROPE_DIM)
        w_pe_a, w_pe_b = w_pe.astype(BF16), _swap_halves(w_pe).astype(BF16)
        w_uk_t = jnp.transpose(w_uk[l], (1, 2, 0)).astype(BF16)
        w_uv_h = jnp.transpose(w_uv[l], (1, 0, 2)).astype(BF16)
        w_out_a = w_out[l][:gw].astype(BF16)
        w_out_b = w_out[l][gw:].astype(BF16)
        w_pq = peer_w_q[l].reshape(dm, p_heads * p_qdim).astype(BF16)
        keys1, keys2 = peer_sub_keys[l, 0], peer_sub_keys[l, 1]
        u_tab = peer_u[l].astype(BF16)
        v_tab_t = peer_v[l].T.astype(BF16)

        def tail(h_in, o_gdn, o_mla):
            h_mid = out_proj(o_gdn, o_mla, w_out_a, w_out_b, h_in)
            pq, xn2 = norm_matmul(h_mid, ffn_norm_g[l], w_pq, emit_xn=True, name="peer_query")
            e1, c1, e2, r2 = peer_topk(pq, keys1, keys2)
            return peer_dense(xn2, u_tab, v_tab_t, e1, c1, e2, r2, h_mid, final_norm_g,
                              final_norm=(l == depth - 1))

        proj = norm_matmul(hp, attn_norm_g[l], w_in_p, name="in_proj")
        o_gdn, s_fin = gdn_prompt(proj, lay, conv_w[l], a_log_l, dt_bias_l, gdn_norm_g[l], nh_g)
        ckv, kpe, ckv_b, kpe_b, qn, qp = mla_prep(proj, lay, q_norm_g[l], kv_norm_g[l], w_nope,
                                                  w_pe_a, w_pe_b, cosq_p, sinq_p, cosk_p, sink_p)
        q_lat = head_proj(qn, w_uk_t, name="q_absorb")
        ctx = attn_prompt(q_lat, qp, ckv_b, kpe_b, nh_m, scale)
        o_mla = head_proj(ctx, w_uv_h, name="v_up")
        outs["sg_p"].append(s_fin[None])
        outs["cv_p"].append(proj[t - (kw - 1):, :3 * gw][None])
        outs["ck_p"].append(ckv[None])
        outs["kp_p"].append(kpe[None])
        hp = tail(hp, o_gdn, o_mla)

        proj_s = norm_matmul(hs, attn_norm_g[l], w_in_p, name="in_proj")
        sc_flat = state_conv[l].reshape(nb, (kw - 1) * 3 * gw)
        qn_s, kn_s, v_s, g_s, sc_new = gdn_sample_prep(proj_s, lay, sc_flat, conv_w[l], a_log_l,
                                                       dt_bias_l, nh_g)
        as_heads = lambda a: a.reshape(nb, nh_g, HEAD_DIM)
        z_s = proj_s[:, 3 * gw:4 * gw]
        s_new, o_gdn_s = gdn_sample_step(state_gdn[l], as_heads(qn_s), as_heads(kn_s), as_heads(v_s),
                                         as_heads(z_s), g_s.reshape(nb, 1, LANES), gdn_norm_g[l])
        ckv_s, kpe_s, _, _, qn2, qp2 = mla_prep(proj_s, lay, q_norm_g[l], kv_norm_g[l], w_nope,
                                                w_pe_a, w_pe_b, cosq_s, sinq_s, cosk_s, sink_s)
        q_lat_s = head_proj(qn2, w_uk_t, name="q_absorb")
        pad_heads = lambda a: jnp.pad(a, ((0, 0), (0, 16 - nh_m), (0, 0)))
        ctx_s = attn_sample(pad_heads(q_lat_s.reshape(nb, nh_m, kv_rank)),
                            pad_heads(jnp.transpose(qp2, (1, 0, 2))),
                            ckv_s.reshape(nb, 1, kv_rank), kpe_s.reshape(nb, 1, ROPE_DIM),
                            cache_ckv, cache_kpe, page_table, l, scale)
        ctx_s = ctx_s[:, :nh_m].reshape(nb, nh_m * kv_rank).astype(BF16)
        o_mla_s = head_proj(ctx_s, w_uv_h, name="v_up")
        outs["sg_s"].append(s_new)
        outs["cv_s"].append(sc_new.reshape(nb, kw - 1, 3 * gw))
        outs["ck_s"].append(ckv_s.reshape(nb, td, kv_rank))
        outs["kp_s"].append(kpe_s.reshape(nb, td, ROPE_DIM))
        hs = tail(hs, o_gdn_s.reshape(nb, gw), o_mla_s)

    st = lambda k: jnp.stack(outs[k])
    return (hp.reshape(b, t, dm), hs.reshape(nb, td, dm), st("sg_p"), st("cv_p"), st("ck_p"), st("kp_p"),
            st("sg_s"), st("cv_s"), st("ck_s"), st("kp_s"))
```
